```python
import jax, jax.numpy as jnp
from jax import lax
import numpy as np

D_MODEL = 1024
BATCH = 32
SEQ = 2048
DEPTH = 2
DEC_BATCH = 4
DEC_SEQ = 4096
PAST_LEN = 128

D_PLE = 256
MIX_WIDTH = D_MODEL
D_CONV = MIX_WIDTH // 2
D_SHORT = MIX_WIDTH - D_CONV
GROUP_DIM = 64
CONV_A_WIDTH = 31
CONV_B_WIDTH = 3
N_IN = 2 * D_CONV + 3 * D_SHORT
PEER_HEADS = 8
N_KEYS = 128
N_EXPERTS = N_KEYS * N_KEYS
D_QUERY = 256
D_HALF = D_QUERY // 2
TOPK_HALF = 16
TOPK = 16
TOKEN_BLOCK = 128
EPS = 1e-6

kernel_name = "hymba_conformer_shortconv_peer_encoder"


def rmsnorm(x, g):
    xf = x.astype(jnp.float32)
    y = xf * lax.rsqrt(jnp.mean(xf * xf, axis=-1, keepdims=True) + EPS)
    return (y * g.astype(jnp.float32)).astype(x.dtype)


def layernorm(x, g, b):
    xf = x.astype(jnp.float32)
    mu = jnp.mean(xf, axis=-1, keepdims=True)
    xc = xf - mu
    y = xc * lax.rsqrt(jnp.mean(xc * xc, axis=-1, keepdims=True) + EPS)
    return (y * g.astype(jnp.float32) + b.astype(jnp.float32)).astype(x.dtype)


def depthwise_conv(x, w):
    k, c = w.shape
    return lax.conv_general_dilated(
        x, w[:, None, :].astype(x.dtype), window_strides=(1,),
        padding=[(k // 2, k // 2)], dimension_numbers=("NWC", "WIO", "NWC"),
        feature_group_count=c)


def token_mixers(n, w_in, conv_a_w, conv_a_b, ln_a_g, ln_a_b, conv_b_w, w_out):
    z = n @ w_in
    a_val, a_gate, b_gate, c_gate, b_in = jnp.split(
        z, [D_CONV, 2 * D_CONV, 2 * D_CONV + D_SHORT, 2 * D_CONV + 2 * D_SHORT], axis=-1)
    a = a_val * jax.nn.sigmoid(a_gate)
    a = depthwise_conv(a, conv_a_w) + conv_a_b
    a = jax.nn.silu(layernorm(a, ln_a_g, ln_a_b))
    hb = b_gate * depthwise_conv(c_gate * b_in, conv_b_w)
    return jnp.concatenate([a, hb], axis=-1) @ w_out


def peer(n, w_q, sub_keys, expert_u, expert_v):
    shp = n.shape
    nf = n.reshape(-1, D_MODEL)
    t = nf.shape[0]
    q = (nf @ w_q).reshape(t, PEER_HEADS, 2, D_HALF)
    s = jnp.einsum("thpc,hpnc->thpn", q, sub_keys).astype(jnp.float32)
    s_top, i_top = lax.top_k(s, TOPK_HALF)
    cand = (s_top[:, :, 0, :, None] + s_top[:, :, 1, None, :]).reshape(t, PEER_HEADS, TOPK_HALF * TOPK_HALF)
    best, j = lax.top_k(cand, TOPK)
    e1 = jnp.take_along_axis(i_top[:, :, 0], j // TOPK_HALF, axis=-1)
    e2 = jnp.take_along_axis(i_top[:, :, 1], j % TOPK_HALF, axis=-1)
    idx = e1 * N_KEYS + e2
    gates = jax.nn.softmax(best, axis=-1)

    nb = t // TOKEN_BLOCK

    def block(args):
        xb, ib, gb = args
        h = jnp.einsum("td,thkd->thk", xb, expert_u[ib])
        act = jax.nn.gelu(h) * gb.astype(h.dtype)
        return jnp.einsum("thk,thkd->td", act, expert_v[ib])

    y = lax.map(block, (nf.reshape(nb, TOKEN_BLOCK, D_MODEL),
                        idx.reshape(nb, TOKEN_BLOCK, PEER_HEADS, TOPK),
                        gates.reshape(nb, TOKEN_BLOCK, PEER_HEADS, TOPK)))
    return y.reshape(shp)


def trunk(x, p, ln1_g, w_in, conv_a_w, conv_a_b, ln_a_g, ln_a_b, conv_b_w, w_out,
          ln2_g, w_q, sub_keys, expert_u, expert_v, w_ple_gate, w_ple_proj, final_g):
    for i in range(DEPTH):
        x = x + token_mixers(rmsnorm(x, ln1_g[i]), w_in[i], conv_a_w[i], conv_a_b[i],
                             ln_a_g[i], ln_a_b[i], conv_b_w[i], w_out[i])
        x = x + peer(rmsnorm(x, ln2_g[i]), w_q[i], sub_keys[i], expert_u[i], expert_v[i])
        x = x + jax.nn.sigmoid(x @ w_ple_gate[i]) * (p[i] @ w_ple_proj[i])
    return rmsnorm(x, final_g)


def setup_inputs(seed: int = 0) -> dict:
    key = jax.random.key(seed)
    ks = jax.random.split(key, 24)
    f32 = jnp.float32

    def nrm(k, shape, scale):
        return jax.random.normal(k, shape, f32) * scale

    def gain(k, shape):
        return 1.0 + 0.1 * jax.random.normal(k, shape, f32)

    return {
        "x_prompt": nrm(ks[0], (BATCH, SEQ, D_MODEL), 1.0),
        "x_sample": nrm(ks[1], (DEC_BATCH, DEC_SEQ, D_MODEL), 1.0),
        "p_prompt": nrm(ks[2], (DEPTH, BATCH, SEQ, D_PLE), 1.0),
        "p_sample": nrm(ks[3], (DEPTH, DEC_BATCH, DEC_SEQ, D_PLE), 1.0),
        "ln1_g": gain(ks[4], (DEPTH, D_MODEL)),
        "w_in": nrm(ks[5], (DEPTH, D_MODEL, N_IN), D_MODEL ** -0.5),
        "conv_a_w": nrm(ks[6], (DEPTH, CONV_A_WIDTH, D_CONV), CONV_A_WIDTH ** -0.5),
        "conv_a_b": nrm(ks[7], (DEPTH, D_CONV), 0.02),
        "ln_a_g": gain(ks[8], (DEPTH, D_CONV)),
        "ln_a_b": nrm(ks[9], (DEPTH, D_CONV), 0.02),
        "conv_b_w": nrm(ks[10], (DEPTH, CONV_B_WIDTH, D_SHORT), CONV_B_WIDTH ** -0.5),
        "w_out": nrm(ks[11], (DEPTH, MIX_WIDTH, D_MODEL), MIX_WIDTH ** -0.5),
        "ln2_g": gain(ks[12], (DEPTH, D_MODEL)),
        "w_q": nrm(ks[13], (DEPTH, D_MODEL, PEER_HEADS * D_QUERY), D_MODEL ** -0.5),
        "sub_keys": nrm(ks[14], (DEPTH, PEER_HEADS, 2, N_KEYS, D_HALF), D_HALF ** -0.5),
        "expert_u": nrm(ks[15], (DEPTH, N_EXPERTS, D_MODEL), D_MODEL ** -0.5),
        "expert_v": nrm(ks[16], (DEPTH, N_EXPERTS, D_MODEL), PEER_HEADS ** -0.5),
        "w_ple_gate": nrm(ks[17], (DEPTH, D_MODEL, D_MODEL), D_MODEL ** -0.5),
        "w_ple_proj": nrm(ks[18], (DEPTH, D_PLE, D_MODEL), D_PLE ** -0.5),
        "final_g": gain(ks[19], (D_MODEL,)),
    }


def reference(x_prompt, x_sample, p_prompt, p_sample, ln1_g, w_in, conv_a_w, conv_a_b,
              ln_a_g, ln_a_b, conv_b_w, w_out, ln2_g, w_q, sub_keys, expert_u, expert_v,
              w_ple_gate, w_ple_proj, final_g):
    y_prompt = trunk(x_prompt, p_prompt, ln1_g, w_in, conv_a_w, conv_a_b, ln_a_g, ln_a_b,
                     conv_b_w, w_out, ln2_g, w_q, sub_keys, expert_u, expert_v,
                     w_ple_gate, w_ple_proj, final_g)
    y_sample = trunk(x_sample, p_sample, ln1_g, w_in, conv_a_w, conv_a_b, ln_a_g, ln_a_b,
                     conv_b_w, w_out, ln2_g, w_q, sub_keys, expert_u, expert_v,
                     w_ple_gate, w_ple_proj, final_g)
    return (y_prompt, y_sample)
```

```python
import functools

import jax
import jax.numpy as jnp
from jax import lax
from jax.experimental import pallas as pl
from jax.experimental.pallas import tpu as pltpu

F32 = jnp.float32
BF16 = jnp.bfloat16
EPS = 1e-6

D_CONV = 512
D_SHORT = 512
CONV_A_WIDTH = 31
CONV_B_WIDTH = 3
PEER_HEADS = 8
N_KEYS = 128
TOPK = 16
HALO = 16
LANES = 128
SEQ_TILE = 512
TOKEN_TILE = 512
SUBLANES = 8
EXPERT_TILE = SUBLANES * N_KEYS
VMEM_LIMIT = 56 * 1024 * 1024


def _rmsnorm(x, g):
    return x * lax.rsqrt(jnp.mean(x * x, axis=-1, keepdims=True) + EPS) * g


def _mixer_kernel(xm_ref, xb_ref, xa_ref, g1_ref, win_ref, caw_ref, cab_ref, lag_ref, lab_ref, cbw_ref,
                  wout_ref, o_ref, a_scr, cb_scr, mix_scr, *, ts, n_tiles):
    j = pl.program_id(1)

    def proj(xblk):
        n = _rmsnorm(xblk, g1_ref[...])
        return jnp.dot(n.astype(BF16), win_ref[...], preferred_element_type=F32)

    def glu(z):
        return z[:, :D_CONV] * jax.nn.sigmoid(z[:, D_CONV:2 * D_CONV])

    def gated_in(z):
        return z[:, 2 * D_CONV + D_SHORT:2 * D_CONV + 2 * D_SHORT] * z[:, 2 * D_CONV + 2 * D_SHORT:]

    zm = proj(xm_ref[0])
    zb = proj(xb_ref[0])
    za = proj(xa_ref[0])
    has_before = (j > 0).astype(F32)
    has_after = (j < n_tiles - 1).astype(F32)
    a_scr[0:HALO, :] = glu(zb) * has_before
    a_scr[HALO:HALO + ts, :] = glu(zm)
    a_scr[HALO + ts:, :] = glu(za) * has_after
    cb_scr[0:HALO, :] = gated_in(zb) * has_before
    cb_scr[HALO:HALO + ts, :] = gated_in(zm)
    cb_scr[HALO + ts:, :] = gated_in(za) * has_after

    rc = 128
    for r0 in range(0, ts, rc):
        for c0 in range(0, D_CONV, LANES):
            cols = slice(c0, c0 + LANES)
            acc = jnp.zeros((rc, LANES), F32)
            off = HALO - CONV_A_WIDTH // 2
            for k in range(CONV_A_WIDTH):
                acc = acc + a_scr[r0 + off + k:r0 + off + k + rc, cols] * caw_ref[k:k + 1, cols]
            mix_scr[r0:r0 + rc, cols] = acc + cab_ref[:, cols]
            accb = jnp.zeros((rc, LANES), F32)
            offb = HALO - CONV_B_WIDTH // 2
            for k in range(CONV_B_WIDTH):
                accb = accb + cb_scr[r0 + offb + k:r0 + offb + k + rc, cols] * cbw_ref[k:k + 1, cols]
            mix_scr[r0:r0 + rc, D_CONV + c0:D_CONV + c0 + LANES] = (
                accb * zm[r0:r0 + rc, 2 * D_CONV + c0:2 * D_CONV + c0 + LANES])

    a = mix_scr[:, :D_CONV]
    mu = jnp.mean(a, axis=-1, keepdims=True)
    ac = a - mu
    a = ac * lax.rsqrt(jnp.mean(ac * ac, axis=-1, keepdims=True) + EPS) * lag_ref[...] + lab_ref[...]
    a = a * jax.nn.sigmoid(a)
    y = jnp.dot(a.astype(BF16), wout_ref[:D_CONV, :], preferred_element_type=F32)
    y = y + jnp.dot(mix_scr[:, D_CONV:].astype(BF16), wout_ref[D_CONV:, :], preferred_element_type=F32)
    o_ref[0] = xm_ref[0] + y


def _mixer(x, g1, w_in, caw, cab, lag, lab, cbw, w_out):
    b, s, d = x.shape
    ts = min(SEQ_TILE, s)
    n_tiles = s // ts
    hb = ts // HALO
    n_in = w_in.shape[1]
    const2 = lambda i, j: (0, 0)
    return pl.pallas_call(
        functools.partial(_mixer_kernel, ts=ts, n_tiles=n_tiles),
        grid=(b, n_tiles),
        in_specs=[
            pl.BlockSpec((1, ts, d), lambda i, j: (i, j, 0)),
            pl.BlockSpec((1, HALO, d), lambda i, j: (i, jnp.maximum(j * hb - 1, 0), 0)),
            pl.BlockSpec((1, HALO, d), lambda i, j: (i, jnp.minimum((j + 1) * hb, s // HALO - 1), 0)),
            pl.BlockSpec((1, d), const2),
            pl.BlockSpec((d, n_in), const2),
            pl.BlockSpec((CONV_A_WIDTH, D_CONV), const2),
            pl.BlockSpec((1, D_CONV), const2),
            pl.BlockSpec((1, D_CONV), const2),
            pl.BlockSpec((1, D_CONV), const2),
            pl.BlockSpec((CONV_B_WIDTH, D_SHORT), const2),
            pl.BlockSpec((D_CONV + D_SHORT, d), const2),
        ],
        out_specs=pl.BlockSpec((1, ts, d), lambda i, j: (i, j, 0)),
        out_shape=jax.ShapeDtypeStruct((b, s, d), F32),
        scratch_shapes=[
            pltpu.VMEM((ts + 2 * HALO, D_CONV), F32),
            pltpu.VMEM((ts + 2 * HALO, D_SHORT), F32),
            pltpu.VMEM((ts, D_CONV + D_SHORT), F32),
        ],
        compiler_params=pltpu.CompilerParams(
            dimension_semantics=("parallel", "parallel"), vmem_limit_bytes=VMEM_LIMIT),
        name="mixer",
    )(x, x, x, g1, w_in, caw, cab, lag, lab, cbw, w_out)


def _top16_sorted(sc, key_iota):
    rank = jnp.full(sc.shape, float(TOPK), F32)
    row16 = lax.broadcasted_iota(jnp.int32, (TOPK, sc.shape[1]), 0)
    vals = jnp.zeros((TOPK, sc.shape[1]), F32)
    for r in range(TOPK):
        m = jnp.max(sc, axis=0, keepdims=True)
        first = jnp.min(jnp.where(sc == m, key_iota, N_KEYS), axis=0, keepdims=True)
        sel = key_iota == first
        rank = jnp.where(sel, float(r), rank)
        sc = jnp.where(sel, -jnp.inf, sc)
        vals = jnp.where(row16 == r, m, vals)
    return rank, vals


def _pair_counts(v1, v2):
    c = v1.shape[1]
    row8 = lax.broadcasted_iota(jnp.int32, (8, c), 0)
    row16 = lax.broadcasted_iota(jnp.int32, (TOPK, c), 0)
    neg = -jnp.inf
    groups = [v1[0:1] + v2]
    flat = [row16]
    for r1 in range(1, 8):
        width = TOPK // (r1 + 1)
        groups.append(jnp.where(row8 < width, v1[r1:r1 + 1] + v2[0:8], neg))
        flat.append(r1 * TOPK + row8)
    groups.append(v1[8:16] + v2[0:1])
    flat.append((8 + row8) * TOPK)
    cand0 = jnp.concatenate(groups, axis=0)
    flat = jnp.concatenate(flat, axis=0)
    cand = cand0
    picked = jnp.zeros(cand.shape, F32)
    for _ in range(TOPK):
        m = jnp.max(cand, axis=0, keepdims=True)
        first = jnp.min(jnp.where(cand == m, flat, TOPK * TOPK), axis=0, keepdims=True)
        sel = flat == first
        picked = jnp.where(sel, 1.0, picked)
        cand = jnp.where(sel, neg, cand)
    top = v1[0:1] + v2[0:1]
    z = jnp.sum(picked * jnp.exp(cand0 - top), axis=0, keepdims=True)
    counts = [jnp.sum(picked[0:16], axis=0, keepdims=True)]
    for r1 in range(1, 8):
        counts.append(jnp.sum(picked[8 + 8 * r1:16 + 8 * r1], axis=0, keepdims=True))
    for r1 in range(8, 16):
        counts.append(picked[72 + r1 - 8:72 + r1 - 7])
    return counts, z


def _route_kernel(x_ref, g2_ref, wqt_ref, keys_ref, n2_ref, rank2_ref, e2_ref, n1_ref, e1n_ref, qt_scr, *, tt):
    nb = _rmsnorm(x_ref[...], g2_ref[...]).astype(BF16)
    n2_ref[...] = nb
    qt_scr[...] = lax.dot_general(wqt_ref[...], nb, (((1,), (1,)), ((), ())),
                                  preferred_element_type=F32).astype(BF16)
    key_iota = lax.broadcasted_iota(jnp.int32, (N_KEYS, LANES), 0)

    def head_body(h, carry):
        s = []
        for p in range(2):
            row0 = pl.multiple_of((2 * h + p) * N_KEYS, N_KEYS)
            s.append(jnp.dot(keys_ref[2 * h + p], qt_scr[pl.ds(row0, N_KEYS), :], preferred_element_type=F32))
        for c0 in range(0, tt, LANES):
            cols = slice(c0, c0 + LANES)
            s1 = s[0][:, cols]
            s2 = s[1][:, cols]
            rank1, v1 = _top16_sorted(s1, key_iota)
            rank2, v2 = _top16_sorted(s2, key_iota)
            counts, z = _pair_counts(v1, v2)
            n1 = jnp.zeros(rank1.shape, F32)
            for r in range(TOPK):
                n1 = jnp.where(rank1 == float(r), counts[r], n1)
            rank2_ref[h, :, cols] = rank2
            n1_ref[h, :, cols] = n1
            e2_ref[h, :, cols] = jnp.exp(s2 - v2[0:1])
            e1n_ref[h, :, cols] = jnp.exp(s1 - v1[0:1]) / z
        return carry

    lax.fori_loop(0, PEER_HEADS, head_body, 0)


def _route(xf, g2, wqt, keys):
    t, d = xf.shape
    tt = min(TOKEN_TILE, t)
    nq = wqt.shape[0]
    code_spec = pl.BlockSpec((PEER_HEADS, N_KEYS, tt), lambda i: (0, 0, i))
    code_shape = jax.ShapeDtypeStruct((PEER_HEADS, N_KEYS, t), F32)
    return pl.pallas_call(
        functools.partial(_route_kernel, tt=tt),
        grid=(t // tt,),
        in_specs=[
            pl.BlockSpec((tt, d), lambda i: (i, 0)),
            pl.BlockSpec((1, d), lambda i: (0, 0)),
            pl.BlockSpec((nq, d), lambda i: (0, 0)),
            pl.BlockSpec((2 * PEER_HEADS, N_KEYS, N_KEYS), lambda i: (0, 0, 0)),
        ],
        out_specs=[pl.BlockSpec((tt, d), lambda i: (i, 0)), code_spec, code_spec, code_spec, code_spec],
        out_shape=[jax.ShapeDtypeStruct((t, d), BF16), code_shape, code_shape, code_shape, code_shape],
        scratch_shapes=[pltpu.VMEM((nq, tt), BF16)],
        compiler_params=pltpu.CompilerParams(dimension_semantics=("parallel",), vmem_limit_bytes=VMEM_LIMIT),
        name="route",
    )(xf, g2, wqt, keys)


def _peer_kernel(n2_ref, u_ref, vt_ref, rank2_ref, e2_ref, n1_ref, e1n_ref, x_ref, p_ref, wg_ref, wp_ref, fg_ref,
                 o_ref, acc_scr, ht_scr, act_scr, *, tt, te, n_expert_tiles, final_norm):
    e = pl.program_id(1)

    @pl.when(e == 0)
    def _():
        acc_scr[...] = jnp.zeros(acc_scr.shape, F32)

    ht_scr[...] = lax.dot_general(u_ref[...], n2_ref[...], (((1,), (1,)), ((), ())), preferred_element_type=F32)
    a0 = pl.multiple_of(e * SUBLANES, SUBLANES)
    for al in range(SUBLANES):
        rows = slice(al * N_KEYS, (al + 1) * N_KEYS)
        for c0 in range(0, tt, LANES):
            cols = slice(c0, c0 + LANES)
            w = jnp.zeros((N_KEYS, LANES), F32)
            for h in range(PEER_HEADS):
                count = n1_ref[h, pl.ds(a0, SUBLANES), cols][al:al + 1]
                scale = e1n_ref[h, pl.ds(a0, SUBLANES), cols][al:al + 1]
                w = w + jnp.where(rank2_ref[h, :, cols] < count, e2_ref[h, :, cols], 0.0) * scale
            act_scr[rows, cols] = (jax.nn.gelu(ht_scr[rows, cols]) * w).astype(BF16)
    acc_scr[...] += jnp.dot(vt_ref[...], act_scr[...], preferred_element_type=F32)

    @pl.when(e == n_expert_tiles - 1)
    def _():
        x1 = x_ref[...] + acc_scr[...].T
        gate = jax.nn.sigmoid(jnp.dot(x1.astype(BF16), wg_ref[...], preferred_element_type=F32))
        x2 = x1 + gate * jnp.dot(p_ref[...].astype(BF16), wp_ref[...], preferred_element_type=F32)
        if final_norm:
            x2 = _rmsnorm(x2, fg_ref[...])
        o_ref[...] = x2


def _peer(n2, u, vt, rank2, e2, n1, e1n, xf, pf, wg, wp, fg, final_norm):
    t, d = xf.shape
    tt = min(TOKEN_TILE, t)
    te = EXPERT_TILE
    n_experts = u.shape[0]
    n_expert_tiles = n_experts // te
    dp = pf.shape[1]
    code_spec = pl.BlockSpec((PEER_HEADS, N_KEYS, tt), lambda i, e: (0, 0, i))
    return pl.pallas_call(
        functools.partial(_peer_kernel, tt=tt, te=te, n_expert_tiles=n_expert_tiles, final_norm=final_norm),
        grid=(t // tt, n_expert_tiles),
        in_specs=[
            pl.BlockSpec((tt, d), lambda i, e: (i, 0)),
            pl.BlockSpec((te, d), lambda i, e: (e, 0)),
            pl.BlockSpec((d, te), lambda i, e: (0, e)),
            code_spec, code_spec, code_spec, code_spec,
            pl.BlockSpec((tt, d), lambda i, e: (i, 0)),
            pl.BlockSpec((tt, dp), lambda i, e: (i, 0)),
            pl.BlockSpec((d, d), lambda i, e: (0, 0)),
            pl.BlockSpec((dp, d), lambda i, e: (0, 0)),
            pl.BlockSpec((1, d), lambda i, e: (0, 0)),
        ],
        out_specs=pl.BlockSpec((tt, d), lambda i, e: (i, 0)),
        out_shape=jax.ShapeDtypeStruct((t, d), F32),
        scratch_shapes=[
            pltpu.VMEM((d, tt), F32),
            pltpu.VMEM((te, tt), F32),
            pltpu.VMEM((te, tt), BF16),
        ],
        compiler_params=pltpu.CompilerParams(
            dimension_semantics=("parallel", "arbitrary"), vmem_limit_bytes=VMEM_LIMIT),
        name="peer",
    )(n2, u, vt, rank2, e2, n1, e1n, xf, pf, wg, wp, fg)


def _trunk(x, p, layers, final_g):
    b, s, d = x.shape
    depth = len(layers)
    for i, w in enumerate(layers):
        x = _mixer(x, w["ln1_g"], w["w_in"], w["conv_a_w"], w["conv_a_b"], w["ln_a_g"], w["ln_a_b"],
                   w["conv_b_w"], w["w_out"])
        xf = x.reshape(b * s, d)
        n2, rank2, e2, n1, e1n = _route(xf, w["ln2_g"], w["w_qt"], w["keys"])
        xf = _peer(n2, w["u"], w["vt"], rank2, e2, n1, e1n, xf, p[i].reshape(b * s, -1),
                   w["w_ple_gate"], w["w_ple_proj"], final_g, final_norm=(i == depth - 1))
        x = xf.reshape(b, s, d)
    return x


def kernel(x_prompt, x_sample, p_prompt, p_sample, ln1_g, w_in, conv_a_w, conv_a_b, ln_a_g, ln_a_b, conv_b_w,
           w_out, ln2_g, w_q, sub_keys, expert_u, expert_v, w_ple_gate, w_ple_proj, final_g):
    depth = w_in.shape[0]
    layers = []
    for i in range(depth):
        layers.append(dict(
            ln1_g=ln1_g[i][None, :],
            w_in=w_in[i].astype(BF16),
            conv_a_w=conv_a_w[i],
            conv_a_b=conv_a_b[i][None, :],
            ln_a_g=ln_a_g[i][None, :],
            ln_a_b=ln_a_b[i][None, :],
            conv_b_w=conv_b_w[i],
            w_out=w_out[i].astype(BF16),
            ln2_g=ln2_g[i][None, :],
            w_qt=w_q[i].T.astype(BF16),
            keys=sub_keys[i].reshape(2 * PEER_HEADS, N_KEYS, -1).astype(BF16),
            u=expert_u[i].astype(BF16),
            vt=expert_v[i].T.astype(BF16),
            w_ple_gate=w_ple_gate[i].astype(BF16),
            w_ple_proj=w_ple_proj[i].astype(BF16),
        ))
    fg = final_g[None, :]
    y_prompt = _trunk(x_prompt, p_prompt, layers, fg)
    y_sample = _trunk(x_sample, p_sample, layers, fg)
    return (y_prompt, y_sample)
```

```python
import functools
import math

import jax
import jax.numpy as jnp
from jax import lax
from jax.experimental import pallas as pl
from jax.experimental.pallas import tpu as pltpu

F32 = jnp.float32
BF16 = jnp.bfloat16
EPS = 1e-6

D_CONV = 512
D_SHORT = 512
CONV_A_WIDTH = 31
CONV_B_WIDTH = 3
PEER_HEADS = 8
N_KEYS = 128
TOPK = 16
HALO = 16
LANES = 128
SUBLANES = 8
BF16_ROWS = 2 * SUBLANES
SEQ_TILE = 512
TOKEN_TILE = 512
EXPERT_TILE = SUBLANES * N_KEYS
EXPERT_SUBTILE = 512
VMEM_LIMIT = 56 * 1024 * 1024
GELU_C0 = math.sqrt(2.0 / math.pi)
GELU_C1 = GELU_C0 * 0.044715


def _rmsnorm(x, g):
    return x * lax.rsqrt(jnp.mean(x * x, axis=-1, keepdims=True) + EPS) * g


def _mixer_kernel(xm_ref, xb_ref, xa_ref, g1_ref, win_ref, caw_ref, cab_ref, lag_ref, lab_ref, cbw_ref,
                  wout_ref, o_ref, a_scr, cb_scr, mix_scr, *, ts, n_tiles):
    j = pl.program_id(1)

    def proj(xblk):
        n = _rmsnorm(xblk, g1_ref[...])
        return jnp.dot(n.astype(BF16), win_ref[...], preferred_element_type=F32)

    def glu(z):
        return z[:, :D_CONV] * jax.nn.sigmoid(z[:, D_CONV:2 * D_CONV])

    def gated_in(z):
        return z[:, 2 * D_CONV + D_SHORT:2 * D_CONV + 2 * D_SHORT] * z[:, 2 * D_CONV + 2 * D_SHORT:]

    zm = proj(xm_ref[0])
    zb = proj(xb_ref[0])
    za = proj(xa_ref[0])
    has_before = (j > 0).astype(F32)
    has_after = (j < n_tiles - 1).astype(F32)
    a_scr[0:HALO, :] = glu(zb) * has_before
    a_scr[HALO:HALO + ts, :] = glu(zm)
    a_scr[HALO + ts:, :] = glu(za) * has_after
    cb_scr[0:HALO, :] = gated_in(zb) * has_before
    cb_scr[HALO:HALO + ts, :] = gated_in(zm)
    cb_scr[HALO + ts:, :] = gated_in(za) * has_after

    rc = 128
    for r0 in range(0, ts, rc):
        for c0 in range(0, D_CONV, LANES):
            cols = slice(c0, c0 + LANES)
            acc = jnp.zeros((rc, LANES), F32)
            off = HALO - CONV_A_WIDTH // 2
            for k in range(CONV_A_WIDTH):
                acc = acc + a_scr[r0 + off + k:r0 + off + k + rc, cols] * caw_ref[k:k + 1, cols]
            mix_scr[r0:r0 + rc, cols] = acc + cab_ref[:, cols]
            accb = jnp.zeros((rc, LANES), F32)
            offb = HALO - CONV_B_WIDTH // 2
            for k in range(CONV_B_WIDTH):
                accb = accb + cb_scr[r0 + offb + k:r0 + offb + k + rc, cols] * cbw_ref[k:k + 1, cols]
            mix_scr[r0:r0 + rc, D_CONV + c0:D_CONV + c0 + LANES] = (
                accb * zm[r0:r0 + rc, 2 * D_CONV + c0:2 * D_CONV + c0 + LANES])

    a = mix_scr[:, :D_CONV]
    mu = jnp.mean(a, axis=-1, keepdims=True)
    ac = a - mu
    a = ac * lax.rsqrt(jnp.mean(ac * ac, axis=-1, keepdims=True) + EPS) * lag_ref[...] + lab_ref[...]
    a = a * jax.nn.sigmoid(a)
    y = jnp.dot(a.astype(BF16), wout_ref[:D_CONV, :], preferred_element_type=F32)
    y = y + jnp.dot(mix_scr[:, D_CONV:].astype(BF16), wout_ref[D_CONV:, :], preferred_element_type=F32)
    o_ref[0] = xm_ref[0] + y


def _mixer(x, g1, w_in, caw, cab, lag, lab, cbw, w_out):
    b, s, d = x.shape
    ts = min(SEQ_TILE, s)
    n_tiles = s // ts
    hb = ts // HALO
    n_in = w_in.shape[1]
    const2 = lambda i, j: (0, 0)
    return pl.pallas_call(
        functools.partial(_mixer_kernel, ts=ts, n_tiles=n_tiles),
        grid=(b, n_tiles),
        in_specs=[
            pl.BlockSpec((1, ts, d), lambda i, j: (i, j, 0)),
            pl.BlockSpec((1, HALO, d), lambda i, j: (i, jnp.maximum(j * hb - 1, 0), 0)),
            pl.BlockSpec((1, HALO, d), lambda i, j: (i, jnp.minimum((j + 1) * hb, s // HALO - 1), 0)),
            pl.BlockSpec((1, d), const2),
            pl.BlockSpec((d, n_in), const2),
            pl.BlockSpec((CONV_A_WIDTH, D_CONV), const2),
            pl.BlockSpec((1, D_CONV), const2),
            pl.BlockSpec((1, D_CONV), const2),
            pl.BlockSpec((1, D_CONV), const2),
            pl.BlockSpec((CONV_B_WIDTH, D_SHORT), const2),
            pl.BlockSpec((D_CONV + D_SHORT, d), const2),
        ],
        out_specs=pl.BlockSpec((1, ts, d), lambda i, j: (i, j, 0)),
        out_shape=jax.ShapeDtypeStruct((b, s, d), F32),
        scratch_shapes=[
            pltpu.VMEM((ts + 2 * HALO, D_CONV), F32),
            pltpu.VMEM((ts + 2 * HALO, D_SHORT), F32),
            pltpu.VMEM((ts, D_CONV + D_SHORT), F32),
        ],
        compiler_params=pltpu.CompilerParams(
            dimension_semantics=("parallel", "parallel"), vmem_limit_bytes=VMEM_LIMIT),
        name="mixer",
    )(x, x, x, g1, w_in, caw, cab, lag, lab, cbw, w_out)


def _top16(sc, want_rank, exact):
    c = sc.shape[1]
    rank = jnp.full(sc.shape, float(TOPK), F32) if want_rank else None
    row16 = lax.broadcasted_iota(jnp.int32, (TOPK, c), 0)
    vals = jnp.zeros((TOPK, c), F32)
    if exact:
        key_iota = lax.broadcasted_iota(jnp.int32, sc.shape, 0)
    for r in range(TOPK):
        m = jnp.max(sc, axis=0, keepdims=True)
        sel = sc == m
        if exact:
            first = jnp.min(jnp.where(sel, key_iota, N_KEYS), axis=0, keepdims=True)
            sel = key_iota == first
        if want_rank:
            rank = jnp.where(sel, float(r), rank)
        sc = jnp.where(sel, -jnp.inf, sc)
        vals = jnp.where(row16 == r, m, vals)
    n_taken = jnp.sum(jnp.where(sc == -jnp.inf, 1.0, 0.0), axis=0, keepdims=True)
    return rank, vals, n_taken


def _pair_counts(v1, v2, exact):
    c = v1.shape[1]
    row8 = lax.broadcasted_iota(jnp.int32, (8, c), 0)
    row16 = lax.broadcasted_iota(jnp.int32, (TOPK, c), 0)
    neg = -jnp.inf
    groups = [v1[0:1] + v2]
    flat = [row16]
    for r1 in range(1, 8):
        width = TOPK // (r1 + 1)
        groups.append(jnp.where(row8 < width, v1[r1:r1 + 1] + v2[0:8], neg))
        flat.append(r1 * TOPK + row8)
    groups.append(v1[8:16] + v2[0:1])
    flat.append((8 + row8) * TOPK)
    cand0 = jnp.concatenate(groups, axis=0)
    flat = jnp.concatenate(flat, axis=0)
    cand = cand0
    picked = jnp.zeros(cand.shape, F32)
    for _ in range(TOPK):
        m = jnp.max(cand, axis=0, keepdims=True)
        sel = cand == m
        if exact:
            first = jnp.min(jnp.where(sel, flat, TOPK * TOPK), axis=0, keepdims=True)
            sel = flat == first
        picked = jnp.where(sel, 1.0, picked)
        cand = jnp.where(sel, neg, cand)
    top = v1[0:1] + v2[0:1]
    z = jnp.sum(picked * jnp.exp(cand0 - top), axis=0, keepdims=True)
    counts = [jnp.sum(picked[0:16], axis=0, keepdims=True)]
    for r1 in range(1, 8):
        counts.append(jnp.sum(picked[8 + 8 * r1:16 + 8 * r1], axis=0, keepdims=True))
    for r1 in range(8, 16):
        counts.append(picked[72 + r1 - 8:72 + r1 - 7])
    n_taken = counts[0]
    for r1 in range(1, 16):
        n_taken = n_taken + counts[r1]
    return counts, z, n_taken


def _route_chunk(s1, s2, exact):
    rank1, v1, taken1 = _top16(s1, want_rank=exact, exact=exact)
    rank2, v2, taken2 = _top16(s2, want_rank=True, exact=exact)
    counts, z, taken12 = _pair_counts(v1, v2, exact)
    n1 = jnp.zeros(s1.shape, F32)
    for r in range(TOPK):
        hit = (rank1 == float(r)) if exact else (s1 == v1[r:r + 1])
        n1 = jnp.where(hit, counts[r], n1)
    e2 = jnp.exp(s2 - v2[0:1])
    e1n = jnp.exp(s1 - v1[0:1]) * (0.5 / z)
    k = float(TOPK)
    bad = jnp.where((taken1 != k) | (taken2 != k) | (taken12 != k), 1.0, 0.0)
    return rank2, e2, n1, e1n, bad


def _route_kernel(x_ref, g2_ref, wqt_ref, keys_ref, n2t_ref, rank2_ref, e2_ref, n1_ref, e1n_ref, qt_scr, s_scr,
                  *, tt):
    nt = _rmsnorm(x_ref[...], g2_ref[...]).T.astype(BF16)
    n2t_ref[...] = nt
    qt_scr[...] = jnp.dot(wqt_ref[...], nt, preferred_element_type=F32).astype(BF16)

    def head_body(h, carry):
        for p in range(2):
            row0 = pl.multiple_of((2 * h + p) * N_KEYS, N_KEYS)
            s_scr[p] = jnp.dot(keys_ref[2 * h + p], qt_scr[pl.ds(row0, N_KEYS), :], preferred_element_type=F32)

        def do_chunks(exact):
            bad = jnp.zeros((1, LANES), F32)
            for c0 in range(0, tt, LANES):
                cols = slice(c0, c0 + LANES)
                rank2, e2, n1, e1n, bad_c = _route_chunk(s_scr[0, :, cols], s_scr[1, :, cols], exact)
                rank2_ref[h, :, cols] = pltpu.bitcast(rank2.astype(BF16), jnp.uint32)
                e2_ref[h, :, cols] = pltpu.bitcast(e2.astype(BF16), jnp.uint32)
                n1_ref[h, :, cols] = n1
                e1n_ref[h, :, cols] = e1n
                bad = jnp.maximum(bad, bad_c)
            return bad

        bad = do_chunks(exact=False)

        @pl.when(jnp.max(bad) > 0.0)
        def _():
            do_chunks(exact=True)

        return carry

    lax.fori_loop(0, PEER_HEADS, head_body, 0)


def _route(xf, g2, wqt, keys):
    t, d = xf.shape
    tt = min(TOKEN_TILE, t)
    nq = wqt.shape[0]
    code_spec = pl.BlockSpec((PEER_HEADS, N_KEYS, tt), lambda i: (0, 0, i))
    pair_spec = pl.BlockSpec((PEER_HEADS, N_KEYS // 2, tt), lambda i: (0, 0, i))
    code_f32 = jax.ShapeDtypeStruct((PEER_HEADS, N_KEYS, t), F32)
    code_pairs = jax.ShapeDtypeStruct((PEER_HEADS, N_KEYS // 2, t), jnp.uint32)
    return pl.pallas_call(
        functools.partial(_route_kernel, tt=tt),
        grid=(t // tt,),
        in_specs=[
            pl.BlockSpec((tt, d), lambda i: (i, 0)),
            pl.BlockSpec((1, d), lambda i: (0, 0)),
            pl.BlockSpec((nq, d), lambda i: (0, 0)),
            pl.BlockSpec((2 * PEER_HEADS, N_KEYS, N_KEYS), lambda i: (0, 0, 0)),
        ],
        out_specs=[pl.BlockSpec((d, tt), lambda i: (0, i)), pair_spec, pair_spec, code_spec, code_spec],
        out_shape=[jax.ShapeDtypeStruct((d, t), BF16), code_pairs, code_pairs, code_f32, code_f32],
        scratch_shapes=[pltpu.VMEM((nq, tt), BF16), pltpu.VMEM((2, N_KEYS, tt), F32)],
        compiler_params=pltpu.CompilerParams(dimension_semantics=("parallel",), vmem_limit_bytes=VMEM_LIMIT),
        name="route",
    )(xf, g2, wqt, keys)


def _peer_kernel(n2t_ref, u_ref, vt_ref, rank2_ref, e2_ref, n1_ref, e1n_ref, x_ref, p_ref, wg_ref, wp_ref, fg_ref,
                 o_ref, acc_scr, ht_scr, act_scr, *, tt, te, n_expert_tiles, final_norm):
    e = pl.program_id(1)

    @pl.when(e == 0)
    def _():
        acc_scr[...] = jnp.zeros(acc_scr.shape, F32)

    sub = EXPERT_SUBTILE
    for s0 in range(0, te, sub):
        ht_scr[s0:s0 + sub, :] = jnp.dot(u_ref[s0:s0 + sub, :], n2t_ref[...], preferred_element_type=F32)
    a0 = pl.multiple_of(e * SUBLANES, SUBLANES)
    zero = jnp.zeros((), BF16)
    for s0 in range(0, te, sub):
        for al in range(s0 // N_KEYS, (s0 + sub) // N_KEYS):
            for c0 in range(0, tt, LANES):
                cols = slice(c0, c0 + LANES)
                n_groups = N_KEYS // BF16_ROWS
                w = [jnp.zeros((BF16_ROWS, LANES), BF16) for _ in range(n_groups)]
                for h in range(PEER_HEADS):
                    count = n1_ref[h, pl.ds(a0, SUBLANES), cols][al:al + 1]
                    scale = e1n_ref[h, pl.ds(a0, SUBLANES), cols][al:al + 1]
                    count = jnp.broadcast_to(count, (BF16_ROWS, LANES)).astype(BF16)
                    scale = jnp.broadcast_to(scale, (BF16_ROWS, LANES)).astype(BF16)
                    for gi in range(n_groups):
                        words = slice(gi * SUBLANES, (gi + 1) * SUBLANES)
                        rank2 = pltpu.bitcast(rank2_ref[h, words, cols], BF16)
                        e2 = pltpu.bitcast(e2_ref[h, words, cols], BF16)
                        w[gi] = w[gi] + jnp.where(rank2 < count, e2, zero) * scale
                for gi in range(n_groups):
                    r0 = al * N_KEYS + gi * BF16_ROWS
                    x = ht_scr[r0:r0 + BF16_ROWS, cols].astype(BF16)
                    g = x * (1.0 + jnp.tanh(x * (GELU_C0 + GELU_C1 * (x * x))))
                    act_scr[r0:r0 + BF16_ROWS, cols] = g * w[gi]
        acc_scr[...] += jnp.dot(vt_ref[:, s0:s0 + sub], act_scr[s0:s0 + sub, :], preferred_element_type=F32)

    @pl.when(e == n_expert_tiles - 1)
    def _():
        x1 = x_ref[...] + acc_scr[...].T
        gate = jax.nn.sigmoid(jnp.dot(x1.astype(BF16), wg_ref[...], preferred_element_type=F32))
        x2 = x1 + gate * jnp.dot(p_ref[...].astype(BF16), wp_ref[...], preferred_element_type=F32)
        if final_norm:
            x2 = _rmsnorm(x2, fg_ref[...])
        o_ref[...] = x2


def _peer(n2t, u, vt, rank2, e2, n1, e1n, xf, pf, wg, wp, fg, final_norm):
    t, d = xf.shape
    tt = min(TOKEN_TILE, t)
    te = EXPERT_TILE
    n_experts = u.shape[0]
    assert n_experts == N_KEYS * N_KEYS and te % EXPERT_SUBTILE == 0
    n_expert_tiles = n_experts // te
    dp = pf.shape[1]
    code_spec = pl.BlockSpec((PEER_HEADS, N_KEYS, tt), lambda i, e: (0, 0, i))
    pair_spec = pl.BlockSpec((PEER_HEADS, N_KEYS // 2, tt), lambda i, e: (0, 0, i))
    return pl.pallas_call(
        functools.partial(_peer_kernel, tt=tt, te=te, n_expert_tiles=n_expert_tiles, final_norm=final_norm),
        grid=(t // tt, n_expert_tiles),
        in_specs=[
            pl.BlockSpec((d, tt), lambda i, e: (0, i)),
            pl.BlockSpec((te, d), lambda i, e: (e, 0)),
            pl.BlockSpec((d, te), lambda i, e: (0, e)),
            pair_spec, pair_spec, code_spec, code_spec,
            pl.BlockSpec((tt, d), lambda i, e: (i, 0)),
            pl.BlockSpec((tt, dp), lambda i, e: (i, 0)),
            pl.BlockSpec((d, d), lambda i, e: (0, 0)),
            pl.BlockSpec((dp, d), lambda i, e: (0, 0)),
            pl.BlockSpec((1, d), lambda i, e: (0, 0)),
        ],
        out_specs=pl.BlockSpec((tt, d), lambda i, e: (i, 0)),
        out_shape=jax.ShapeDtypeStruct((t, d), F32),
        scratch_shapes=[
            pltpu.VMEM((d, tt), F32),
            pltpu.VMEM((te, tt), F32),
            pltpu.VMEM((te, tt), BF16),
        ],
        compiler_params=pltpu.CompilerParams(
            dimension_semantics=("parallel", "arbitrary"), vmem_limit_bytes=VMEM_LIMIT),
        name="peer",
    )(n2t, u, vt, rank2, e2, n1, e1n, xf, pf, wg, wp, fg)


def _trunk(x, p, layers, final_g):
    b, s, d = x.shape
    depth = len(layers)
    for i, w in enumerate(layers):
        x = _mixer(x, w["ln1_g"], w["w_in"], w["conv_a_w"], w["conv_a_b"], w["ln_a_g"], w["ln_a_b"],
                   w["conv_b_w"], w["w_out"])
        xf = x.reshape(b * s, d)
        n2t, rank2, e2, n1, e1n = _route(xf, w["ln2_g"], w["w_qt"], w["keys"])
        xf = _peer(n2t, w["u"], w["vt"], rank2, e2, n1, e1n, xf, p[i].reshape(b * s, -1),
                   w["w_ple_gate"], w["w_ple_proj"], final_g, final_norm=(i == depth - 1))
        x = xf.reshape(b, s, d)
    return x


def kernel(x_prompt, x_sample, p_prompt, p_sample, ln1_g, w_in, conv_a_w, conv_a_b, ln_a_g, ln_a_b, conv_b_w,
           w_out, ln2_g, w_q, sub_keys, expert_u, expert_v, w_ple_gate, w_ple_proj, final_g):
    depth = w_in.shape[0]
    layers = []
    for i in range(depth):
        layers.append(dict(
            ln1_g=ln1_g[i][None, :],
            w_in=w_in[i].astype(BF16),
            conv_a_w=conv_a_w[i],
            conv_a_b=conv_a_b[i][None, :],
            ln_a_g=ln_a_g[i][None, :],
            ln_a_b=ln_a_b[i][None, :],
            conv_b_w=conv_b_w[i],
            w_out=w_out[i].astype(BF16),
            ln2_g=ln2_g[i][None, :],
            w_qt=w_q[i].T.astype(BF16),
            keys=sub_keys[i].reshape(2 * PEER_HEADS, N_KEYS, -1).astype(BF16),
            u=expert_u[i].astype(BF16),
            vt=expert_v[i].T.astype(BF16),
            w_ple_gate=w_ple_gate[i].astype(BF16),
            w_ple_proj=w_ple_proj[i].astype(BF16),
        ))
    fg = final_g[None, :]
    y_prompt = _trunk(x_prompt, p_prompt, layers, fg)
    y_sample = _trunk(x_sample, p_sample, layers, fg)
    return (y_prompt, y_sample)
```
